```python
import jax, jax.numpy as jnp
from jax import lax
import numpy as np

D_MODEL = 4096
BATCH = 2
SEQ = 8192
DEPTH = 4

HEAD_DIM = 64
N_Q_HEADS = D_MODEL // 128
KV_RATIO = 8
N_KV_HEADS = N_Q_HEADS // KV_RATIO
ATTN_WIDTH = N_Q_HEADS * HEAD_DIM
KV_WIDTH = N_KV_HEADS * HEAD_DIM
CONV_WIDTH = D_MODEL // 4
CONV_K = 3
WINDOW = 128
BLOCK = 128
ROPE_THETA = 500000.0
ROT_DIM = HEAD_DIM // 4
D_FF = 4 * D_MODEL
ALPHA = (2 * DEPTH) ** 0.25
BETA = (8 * DEPTH) ** -0.25
LN_EPS = 1e-5
NEG_INF = -1e30
IN_WIDTHS = (ATTN_WIDTH, KV_WIDTH, KV_WIDTH, CONV_WIDTH, CONV_WIDTH, CONV_WIDTH, D_MODEL, D_MODEL)
IN_WIDTH = ATTN_WIDTH + 2 * KV_WIDTH + 3 * CONV_WIDTH + 2 * D_MODEL

kernel_name = "hybrid_swa_sink_shortconv_gated_deepnorm"


def layer_norm(x, g, b):
    xf = x.astype(jnp.float32)
    mu = jnp.mean(xf, axis=-1, keepdims=True)
    var = jnp.mean(jnp.square(xf - mu), axis=-1, keepdims=True)
    y = (xf - mu) * lax.rsqrt(var + LN_EPS) * g.astype(jnp.float32) + b.astype(jnp.float32)
    return y.astype(x.dtype)


def rope_tables(positions):
    inv_freq = ROPE_THETA ** (-jnp.arange(0, ROT_DIM, 2, dtype=jnp.float32) / ROT_DIM)
    ang = positions.astype(jnp.float32)[..., None] * inv_freq
    return jnp.cos(ang)[:, :, None, :], jnp.sin(ang)[:, :, None, :]


def apply_partial_rope(t, cos, sin):
    half = ROT_DIM // 2
    rot = t[..., :ROT_DIM].astype(jnp.float32)
    t1, t2 = rot[..., :half], rot[..., half:]
    rotated = jnp.concatenate([t1 * cos - t2 * sin, t2 * cos + t1 * sin], axis=-1)
    return jnp.concatenate([rotated.astype(t.dtype), t[..., ROT_DIM:]], axis=-1)


def sliding_window_sink_attention(q, k, v, sinks):
    b, s = q.shape[0], q.shape[1]
    nblk = s // BLOCK
    qb = q.reshape(b, nblk, BLOCK, N_KV_HEADS, KV_RATIO, HEAD_DIM)

    def band(t):
        tb = t.reshape(b, nblk, BLOCK, N_KV_HEADS, HEAD_DIM)
        prev = jnp.pad(tb, ((0, 0), (1, 0), (0, 0), (0, 0), (0, 0)))[:, :-1]
        return jnp.concatenate([prev, tb], axis=2)

    kb, vb = band(k), band(v)
    scores = jnp.einsum('bnqhgd,bnkhd->bnhgqk', qb, kb).astype(jnp.float32) * (HEAD_DIM ** -0.5)
    qi = jnp.arange(BLOCK)[:, None]
    kj = jnp.arange(2 * BLOCK)[None, :]
    rel = qi + BLOCK - kj
    in_window = (rel >= 0) & (rel < WINDOW)
    real_key = (jnp.arange(nblk)[:, None] > 0) | (kj >= BLOCK)
    mask = in_window[None] & real_key[:, None, :]
    scores = jnp.where(mask[None, :, None, None], scores, NEG_INF)
    sink = sinks.astype(jnp.float32).reshape(1, 1, N_KV_HEADS, KV_RATIO, 1, 1)
    sink = jnp.broadcast_to(sink, scores.shape[:-1] + (1,))
    probs = jax.nn.softmax(jnp.concatenate([scores, sink], axis=-1), axis=-1)[..., :-1]
    out = jnp.einsum('bnhgqk,bnkhd->bnqhgd', probs.astype(v.dtype), vb)
    return out.reshape(b, s, ATTN_WIDTH)


def gated_short_conv(b_gate, c_gate, h, w):
    s = h.shape[1]
    u = c_gate * h
    u_pad = jnp.pad(u, ((0, 0), (CONV_K - 1, 0), (0, 0)))
    conv = w[0] * u_pad[:, 0:s]
    for tap in range(1, CONV_K):
        conv = conv + w[tap] * u_pad[:, tap:tap + s]
    return b_gate * conv


def setup_inputs(seed: int = 0) -> dict:
    key = jax.random.key(seed)
    ks = jax.random.split(key, 16)
    f32 = jnp.float32
    x = jax.random.normal(ks[0], (BATCH, SEQ, D_MODEL), f32)
    offset = jax.random.randint(ks[1], (BATCH, 1), 0, 4096, dtype=jnp.int32)
    positions = (offset + jnp.arange(SEQ, dtype=jnp.int32)[None, :]).astype(jnp.int32)
    v0 = ATTN_WIDTH + KV_WIDTH
    h0 = ATTN_WIDTH + 2 * KV_WIDTH + 2 * CONV_WIDTH
    col_scale = jnp.ones((IN_WIDTH,), f32).at[v0:v0 + KV_WIDTH].set(BETA).at[h0:h0 + CONV_WIDTH].set(BETA)
    w_in = jax.random.normal(ks[2], (DEPTH, D_MODEL, IN_WIDTH), f32) * (D_MODEL ** -0.5) * col_scale
    conv_w = jax.random.normal(ks[3], (DEPTH, CONV_K, CONV_WIDTH), f32) * (CONV_K ** -0.5)
    attn_sinks = jax.random.normal(ks[4], (DEPTH, N_Q_HEADS), f32) * 0.5
    w_br_attn = jax.random.normal(ks[5], (DEPTH, ATTN_WIDTH, D_MODEL), f32) * (ATTN_WIDTH ** -0.5) * BETA
    w_br_conv = jax.random.normal(ks[6], (DEPTH, CONV_WIDTH, D_MODEL), f32) * (CONV_WIDTH ** -0.5) * BETA
    w_o = jax.random.normal(ks[7], (DEPTH, D_MODEL, D_MODEL), f32) * (D_MODEL ** -0.5) * BETA
    ln1_g = 1.0 + 0.02 * jax.random.normal(ks[8], (DEPTH, D_MODEL), f32)
    ln1_b = 0.02 * jax.random.normal(ks[9], (DEPTH, D_MODEL), f32)
    w_up = jax.random.normal(ks[10], (DEPTH, D_MODEL, D_FF), f32) * (D_MODEL ** -0.5)
    w_down = jax.random.normal(ks[11], (DEPTH, D_FF, D_MODEL), f32) * (D_FF ** -0.5) * BETA
    ln2_g = 1.0 + 0.02 * jax.random.normal(ks[12], (DEPTH, D_MODEL), f32)
    ln2_b = 0.02 * jax.random.normal(ks[13], (DEPTH, D_MODEL), f32)
    return {"x": x, "positions": positions, "w_in": w_in, "conv_w": conv_w,
            "attn_sinks": attn_sinks, "w_br_attn": w_br_attn, "w_br_conv": w_br_conv,
            "w_o": w_o, "ln1_g": ln1_g, "ln1_b": ln1_b, "w_up": w_up, "w_down": w_down,
            "ln2_g": ln2_g, "ln2_b": ln2_b}


def reference(x, positions, w_in, conv_w, attn_sinks, w_br_attn, w_br_conv, w_o,
              ln1_g, ln1_b, w_up, w_down, ln2_g, ln2_b):
    b, s = x.shape[0], x.shape[1]
    cos, sin = rope_tables(positions)
    split_points = [int(p) for p in np.cumsum(IN_WIDTHS)[:-1]]
    for l in range(DEPTH):
        proj = x @ w_in[l]
        q, k, v, cb, cc, ch, ga, gc = jnp.split(proj, split_points, axis=-1)
        q = apply_partial_rope(q.reshape(b, s, N_Q_HEADS, HEAD_DIM), cos, sin)
        k = apply_partial_rope(k.reshape(b, s, N_KV_HEADS, HEAD_DIM), cos, sin)
        v = v.reshape(b, s, N_KV_HEADS, HEAD_DIM)
        attn = sliding_window_sink_attention(q, k, v, attn_sinks[l])
        conv = gated_short_conv(cb, cc, ch, conv_w[l])
        merged = jax.nn.sigmoid(ga) * (attn @ w_br_attn[l]) + jax.nn.sigmoid(gc) * (conv @ w_br_conv[l])
        x = layer_norm(ALPHA * x + merged @ w_o[l], ln1_g[l], ln1_b[l])
        hidden = jnp.square(jax.nn.relu(x @ w_up[l]))
        x = layer_norm(ALPHA * x + hidden @ w_down[l], ln2_g[l], ln2_b[l])
    return x
```

```python
import functools

import jax
import jax.numpy as jnp
from jax import lax
from jax.experimental import pallas as pl
from jax.experimental.pallas import tpu as pltpu

HEAD_DIM = 64
KV_RATIO = 8
BLOCK = 128
ROPE_THETA = 500000.0
ROT_DIM = HEAD_DIM // 4
CONV_K = 3
LN_EPS = 1e-5
NEG_INF = -1e30
LANES = 128
SUBLANES = 8
VMEM_LIMIT_BYTES = 56 * 1024 * 1024

BF16 = jnp.bfloat16
F32 = jnp.float32


def _params(sem):
    return pltpu.CompilerParams(dimension_semantics=sem, vmem_limit_bytes=VMEM_LIMIT_BYTES)


def _inproj_kernel(l_ref, x_ref, w_ref, c_ref, s1_ref, s2_ref, o_ref, *, n_q_tiles, n_plain_end):
    j = pl.program_id(1)
    acc = jnp.dot(x_ref[...], w_ref[...], preferred_element_type=F32)
    bn = acc.shape[1]
    n_slabs = bn // LANES

    def rope(a):
        half = ROT_DIM // 2
        return (a * c_ref[...] + pltpu.roll(a, half, 1) * s1_ref[...]
                + pltpu.roll(a, LANES - half, 1) * s2_ref[...])

    @pl.when(j < n_q_tiles)
    def _():
        for s in range(n_slabs):
            sl = slice(s * LANES, (s + 1) * LANES)
            o_ref[:, sl] = (rope(acc[:, sl]) * (HEAD_DIM ** -0.5)).astype(o_ref.dtype)

    @pl.when(j == n_q_tiles)
    def _():
        for s in range(n_slabs // 2):
            sl = slice(s * LANES, (s + 1) * LANES)
            o_ref[:, sl] = rope(acc[:, sl]).astype(o_ref.dtype)
        o_ref[:, bn // 2:] = acc[:, bn // 2:].astype(o_ref.dtype)

    @pl.when((j > n_q_tiles) & (j < n_plain_end))
    def _():
        o_ref[...] = acc.astype(o_ref.dtype)

    @pl.when(j >= n_plain_end)
    def _():
        o_ref[...] = jax.nn.sigmoid(acc).astype(o_ref.dtype)


def _inproj(l, x, w_in, rope_c, rope_s1, rope_s2, *, attn_width, kv_width, conv_width, bm, bn):
    m, d = x.shape
    n_in = w_in.shape[2]
    assert attn_width % bn == 0 and 2 * kv_width == bn and conv_width % bn == 0
    n_q_tiles = attn_width // bn
    n_plain_end = n_q_tiles + 1 + 3 * conv_width // bn
    n_tiles = n_in // bn
    out_w = n_in + bn
    return pl.pallas_call(
        functools.partial(_inproj_kernel, n_q_tiles=n_q_tiles, n_plain_end=n_plain_end),
        grid_spec=pltpu.PrefetchScalarGridSpec(
            num_scalar_prefetch=1,
            grid=(m // bm, n_tiles),
            in_specs=[
                pl.BlockSpec((bm, d), lambda i, j, l: (i, 0)),
                pl.BlockSpec((None, d, bn), lambda i, j, l: (l[0], 0, j)),
                pl.BlockSpec((bm, LANES), lambda i, j, l: (i, 0)),
                pl.BlockSpec((bm, LANES), lambda i, j, l: (i, 0)),
                pl.BlockSpec((bm, LANES), lambda i, j, l: (i, 0)),
            ],
            out_specs=pl.BlockSpec((bm, bn), lambda i, j, l: (i, jnp.where(j > n_q_tiles, j + 1, j))),
        ),
        out_shape=jax.ShapeDtypeStruct((m, out_w), BF16),
        compiler_params=_params(("arbitrary", "arbitrary")),
        name="inproj",
    )(l, x, w_in, rope_c, rope_s1, rope_s2)


def _attn_kernel(l_ref, q_ref, kvo_ref, kvp_ref, sink_ref, o_ref, *, nblk, n_kv_heads):
    r = pl.program_id(0)
    n = r % nblk
    layer = l_ref[0]
    kv_w = n_kv_heads * HEAD_DIM
    kv = jnp.concatenate([kvp_ref[...], kvo_ref[...]], axis=0).astype(F32)

    qi = lax.broadcasted_iota(jnp.int32, (BLOCK, 2 * BLOCK), 0)
    kj = lax.broadcasted_iota(jnp.int32, (BLOCK, 2 * BLOCK), 1)
    rel = qi + BLOCK - kj
    mask = (rel >= 0) & (rel < BLOCK) & ((n > 0) | (kj >= BLOCK))
    low_half = lax.broadcasted_iota(jnp.int32, (2 * BLOCK, LANES), 1) < HEAD_DIM

    def head_at(slab, odd_src, odd_dst):
        moved = slab if odd_src == odd_dst else pltpu.roll(slab, HEAD_DIM, 1)
        keep = jnp.logical_not(low_half) if odd_dst else low_half
        return jnp.where(keep, moved, 0.0).astype(BF16)

    for kvh in range(n_kv_heads):
        ks = kv[:, (kvh // 2) * LANES:(kvh // 2 + 1) * LANES]
        vs = kv[:, kv_w + (kvh // 2) * LANES:kv_w + (kvh // 2 + 1) * LANES]
        kz = [head_at(ks, kvh % 2, p) for p in range(2)]
        vz = [head_at(vs, kvh % 2, p) for p in range(2)]
        pair = None
        for g in range(KV_RATIO):
            h = kvh * KV_RATIO + g
            p, slab = h % 2, h // 2
            qs = q_ref[:, slab * LANES:(slab + 1) * LANES]
            s = lax.dot_general(qs, kz[p], (((1,), (1,)), ((), ())), preferred_element_type=F32)
            s = jnp.where(mask, s, NEG_INF)
            sink = sink_ref[layer, h]
            mx = jnp.maximum(jnp.max(s, axis=1, keepdims=True), sink)
            e = jnp.exp(s - mx)
            den = jnp.sum(e, axis=1, keepdims=True) + jnp.exp(sink - mx)
            probs = (e * (1.0 / den)).astype(BF16)
            contrib = jnp.dot(probs, vz[p], preferred_element_type=F32)
            if p == 0:
                pair = contrib
            else:
                o_ref[:, slab * LANES:(slab + 1) * LANES] = (pair + contrib).astype(o_ref.dtype)


def _attention(l, proj, sinks, *, seq, attn_width, kv_width):
    m = proj.shape[0]
    nblk = seq // BLOCK
    n_kv_heads = kv_width // HEAD_DIM
    assert n_kv_heads % 2 == 0 and attn_width == n_kv_heads * KV_RATIO * HEAD_DIM
    kv_col = attn_width // (2 * kv_width)
    return pl.pallas_call(
        functools.partial(_attn_kernel, nblk=nblk, n_kv_heads=n_kv_heads),
        grid_spec=pltpu.PrefetchScalarGridSpec(
            num_scalar_prefetch=1,
            grid=(m // BLOCK,),
            in_specs=[
                pl.BlockSpec((BLOCK, attn_width), lambda r, l: (r, 0)),
                pl.BlockSpec((BLOCK, 2 * kv_width), lambda r, l: (r, kv_col)),
                pl.BlockSpec((BLOCK, 2 * kv_width), lambda r, l: (jnp.maximum(r - 1, 0), kv_col)),
                pl.BlockSpec(memory_space=pltpu.SMEM),
            ],
            out_specs=pl.BlockSpec((BLOCK, attn_width), lambda r, l: (r, 0)),
        ),
        out_shape=jax.ShapeDtypeStruct((m, attn_width), BF16),
        compiler_params=_params(("arbitrary",)),
        name="attn",
    )(l, proj, proj, proj, sinks)


def _merge_kernel(l_ref, attn_ref, cb_ref, cc_ref, ch_ref, hcc_ref, hch_ref, cw_ref, wa_ref, wc_ref,
                  ga_ref, gc_ref, o_ref, conv_ref, *, tiles_per_seq):
    i = pl.program_id(0)
    j = pl.program_id(1)

    @pl.when(j == 0)
    def _():
        u = cc_ref[...].astype(F32) * ch_ref[...].astype(F32)
        first = (i % tiles_per_seq) == 0
        halo = hcc_ref[...].astype(F32) * hch_ref[...].astype(F32)
        halo = jnp.where(first, 0.0, halo)
        row = lax.broadcasted_iota(jnp.int32, u.shape, 0)
        cw = cw_ref[...]
        conv = cw[CONV_K - 1:CONV_K, :] * u
        for back in range(1, CONV_K):
            shifted = pltpu.roll(u, back, 0)
            for t in range(back):
                src = SUBLANES - back + t
                shifted = jnp.where(row == t, halo[src:src + 1, :], shifted)
            conv = conv + cw[CONV_K - 1 - back:CONV_K - back, :] * shifted
        conv_ref[...] = (cb_ref[...].astype(F32) * conv).astype(conv_ref.dtype)

    a = jnp.dot(attn_ref[...], wa_ref[...], preferred_element_type=F32)
    c = jnp.dot(conv_ref[...], wc_ref[...], preferred_element_type=F32)
    o_ref[...] = (ga_ref[...].astype(F32) * a + gc_ref[...].astype(F32) * c).astype(o_ref.dtype)


def _merge(l, attn, proj, conv_w, w_br_attn, w_br_conv, *, seq, d_model, attn_width, conv_width, cb_col, bm, bn):
    m = attn.shape[0]
    assert seq % bm == 0 and cb_col % conv_width == 0 and (cb_col + 3 * conv_width) % bn == 0
    cb_blk = cb_col // conv_width
    ga_blk = (cb_col + 3 * conv_width) // bn
    gc_blk = ga_blk + d_model // bn
    halo_row = lambda i: jnp.maximum(i * (bm // SUBLANES) - 1, 0)
    return pl.pallas_call(
        functools.partial(_merge_kernel, tiles_per_seq=seq // bm),
        grid_spec=pltpu.PrefetchScalarGridSpec(
            num_scalar_prefetch=1,
            grid=(m // bm, d_model // bn),
            in_specs=[
                pl.BlockSpec((bm, attn_width), lambda i, j, l: (i, 0)),
                pl.BlockSpec((bm, conv_width), lambda i, j, l: (i, cb_blk)),
                pl.BlockSpec((bm, conv_width), lambda i, j, l: (i, cb_blk + 1)),
                pl.BlockSpec((bm, conv_width), lambda i, j, l: (i, cb_blk + 2)),
                pl.BlockSpec((SUBLANES, conv_width), lambda i, j, l: (halo_row(i), cb_blk + 1)),
                pl.BlockSpec((SUBLANES, conv_width), lambda i, j, l: (halo_row(i), cb_blk + 2)),
                pl.BlockSpec((None, CONV_K, conv_width), lambda i, j, l: (l[0], 0, 0)),
                pl.BlockSpec((None, attn_width, bn), lambda i, j, l: (l[0], 0, j)),
                pl.BlockSpec((None, conv_width, bn), lambda i, j, l: (l[0], 0, j)),
                pl.BlockSpec((bm, bn), lambda i, j, l: (i, ga_blk + j)),
                pl.BlockSpec((bm, bn), lambda i, j, l: (i, gc_blk + j)),
            ],
            out_specs=pl.BlockSpec((bm, bn), lambda i, j, l: (i, j)),
            scratch_shapes=[pltpu.VMEM((bm, conv_width), BF16)],
        ),
        out_shape=jax.ShapeDtypeStruct((m, d_model), BF16),
        compiler_params=_params(("arbitrary", "arbitrary")),
        name="merge",
    )(l, attn, proj, proj, proj, proj, proj, conv_w, w_br_attn, w_br_conv, proj, proj)


def _layer_norm_store(load_chunk, n_chunks, g_ref, b_ref, o_ref):
    d = o_ref.shape[1]
    w = d // n_chunks
    total = load_chunk(0).sum(axis=1, keepdims=True)
    for c in range(1, n_chunks):
        total = total + load_chunk(c).sum(axis=1, keepdims=True)
    mu = total * (1.0 / d)
    sq = jnp.square(load_chunk(0) - mu).sum(axis=1, keepdims=True)
    for c in range(1, n_chunks):
        sq = sq + jnp.square(load_chunk(c) - mu).sum(axis=1, keepdims=True)
    rstd = lax.rsqrt(sq * (1.0 / d) + LN_EPS)
    for c in range(n_chunks):
        sl = slice(c * w, (c + 1) * w)
        o_ref[:, sl] = ((load_chunk(c) - mu) * rstd * g_ref[:, sl] + b_ref[:, sl]).astype(o_ref.dtype)


def _oproj_kernel(l_ref, a_ref, w_ref, xres_ref, g_ref, b_ref, o_ref, y_ref, *, alpha, n_tiles):
    j = pl.program_id(1)
    y_ref[j] = alpha * xres_ref[...].astype(F32) + jnp.dot(a_ref[...], w_ref[...], preferred_element_type=F32)

    @pl.when(j == n_tiles - 1)
    def _():
        _layer_norm_store(lambda c: y_ref[c], n_tiles, g_ref, b_ref, o_ref)


def _oproj_ln(l, merged, w_o, xres, g, b, *, alpha, bm, bn):
    m, d = merged.shape
    n_tiles = d // bn
    return pl.pallas_call(
        functools.partial(_oproj_kernel, alpha=alpha, n_tiles=n_tiles),
        grid_spec=pltpu.PrefetchScalarGridSpec(
            num_scalar_prefetch=1,
            grid=(m // bm, n_tiles),
            in_specs=[
                pl.BlockSpec((bm, d), lambda i, j, l: (i, 0)),
                pl.BlockSpec((None, d, bn), lambda i, j, l: (l[0], 0, j)),
                pl.BlockSpec((bm, bn), lambda i, j, l: (i, j)),
                pl.BlockSpec((None, 1, d), lambda i, j, l: (l[0], 0, 0)),
                pl.BlockSpec((None, 1, d), lambda i, j, l: (l[0], 0, 0)),
            ],
            out_specs=pl.BlockSpec((bm, d), lambda i, j, l: (i, 0)),
            scratch_shapes=[pltpu.VMEM((n_tiles, bm, bn), F32)],
        ),
        out_shape=jax.ShapeDtypeStruct((m, d), BF16),
        compiler_params=_params(("arbitrary", "arbitrary")),
        name="oproj_ln",
    )(l, merged, w_o, xres, g, b)


def _up_kernel(l_ref, x_ref, w_ref, o_ref):
    h = jnp.maximum(jnp.dot(x_ref[...], w_ref[...], preferred_element_type=F32), 0.0)
    o_ref[...] = (h * h).astype(o_ref.dtype)


def _up(l, x, w_up, *, bm, bn):
    m, d = x.shape
    d_ff = w_up.shape[2]
    return pl.pallas_call(
        _up_kernel,
        grid_spec=pltpu.PrefetchScalarGridSpec(
            num_scalar_prefetch=1,
            grid=(m // bm, d_ff // bn),
            in_specs=[
                pl.BlockSpec((bm, d), lambda i, j, l: (i, 0)),
                pl.BlockSpec((None, d, bn), lambda i, j, l: (l[0], 0, j)),
            ],
            out_specs=pl.BlockSpec((bm, bn), lambda i, j, l: (i, j)),
        ),
        out_shape=jax.ShapeDtypeStruct((m, d_ff), BF16),
        compiler_params=_params(("arbitrary", "arbitrary")),
        name="up",
    )(l, x, w_up)


def _down_kernel(l_ref, h_ref, w_ref, xres_ref, g_ref, b_ref, o_ref, acc_ref, *, alpha, n_k, n_chunks):
    k = pl.program_id(1)
    w = acc_ref.shape[1] // n_chunks

    @pl.when(k == 0)
    def _():
        acc_ref[...] = alpha * xres_ref[...].astype(F32)

    for c in range(n_chunks):
        sl = slice(c * w, (c + 1) * w)
        acc_ref[:, sl] += jnp.dot(h_ref[...], w_ref[:, sl], preferred_element_type=F32)

    @pl.when(k == n_k - 1)
    def _():
        _layer_norm_store(lambda c: acc_ref[:, c * w:(c + 1) * w], n_chunks, g_ref, b_ref, o_ref)


def _down_ln(l, hidden, w_down, xres, g, b, *, alpha, bm, bk):
    m, d_ff = hidden.shape
    d = xres.shape[1]
    n_k = d_ff // bk
    return pl.pallas_call(
        functools.partial(_down_kernel, alpha=alpha, n_k=n_k, n_chunks=4),
        grid_spec=pltpu.PrefetchScalarGridSpec(
            num_scalar_prefetch=1,
            grid=(m // bm, n_k),
            in_specs=[
                pl.BlockSpec((bm, bk), lambda i, k, l: (i, k)),
                pl.BlockSpec((None, bk, d), lambda i, k, l: (l[0], k, 0)),
                pl.BlockSpec((bm, d), lambda i, k, l: (i, 0)),
                pl.BlockSpec((None, 1, d), lambda i, k, l: (l[0], 0, 0)),
                pl.BlockSpec((None, 1, d), lambda i, k, l: (l[0], 0, 0)),
            ],
            out_specs=pl.BlockSpec((bm, d), lambda i, k, l: (i, 0)),
            scratch_shapes=[pltpu.VMEM((bm, d), F32)],
        ),
        out_shape=jax.ShapeDtypeStruct((m, d), BF16),
        compiler_params=_params(("arbitrary", "arbitrary")),
        name="down_ln",
    )(l, hidden, w_down, xres, g, b)


def _rope_lane_tables(positions):
    half = ROT_DIM // 2
    inv_freq = ROPE_THETA ** (-jnp.arange(0, ROT_DIM, 2, dtype=F32) / ROT_DIM)
    ang = positions.astype(F32).reshape(-1, 1) * inv_freq
    cos, sin = jnp.cos(ang), jnp.sin(ang)
    m = ang.shape[0]
    zeros = lambda w: jnp.zeros((m, w), F32)
    c64 = jnp.concatenate([cos, cos, jnp.ones((m, HEAD_DIM - ROT_DIM), F32)], axis=1)
    s1_64 = jnp.concatenate([zeros(half), sin, zeros(HEAD_DIM - ROT_DIM)], axis=1)
    s2_64 = jnp.concatenate([-sin, zeros(HEAD_DIM - half)], axis=1)
    rep = LANES // HEAD_DIM
    return jnp.tile(c64, (1, rep)), jnp.tile(s1_64, (1, rep)), jnp.tile(s2_64, (1, rep))


def kernel(x, positions, w_in, conv_w, attn_sinks, w_br_attn, w_br_conv, w_o, ln1_g, ln1_b, w_up, w_down, ln2_g, ln2_b):
    batch, seq, d_model = x.shape
    depth = w_in.shape[0]
    m = batch * seq
    attn_width = w_br_attn.shape[1]
    conv_width = w_br_conv.shape[1]
    kv_width = attn_width // KV_RATIO
    assert w_in.shape[2] == attn_width + 2 * kv_width + 3 * conv_width + 2 * d_model
    alpha = (2 * depth) ** 0.25

    bn_in = 2 * kv_width
    cb_col = attn_width + 2 * kv_width + bn_in
    bm_big = min(1024, seq)
    bm_ln = min(512, seq)

    rope_c, rope_s1, rope_s2 = _rope_lane_tables(positions)
    w_in_b, w_ba_b, w_bc_b, w_o_b, w_up_b, w_down_b = (
        w.astype(BF16) for w in (w_in, w_br_attn, w_br_conv, w_o, w_up, w_down))
    g1, b1, g2, b2 = (t.reshape(depth, 1, d_model) for t in (ln1_g, ln1_b, ln2_g, ln2_b))

    def layer(li, xs):
        l = jnp.reshape(li, (1,)).astype(jnp.int32)
        proj = _inproj(l, xs, w_in_b, rope_c, rope_s1, rope_s2, attn_width=attn_width, kv_width=kv_width,
                       conv_width=conv_width, bm=bm_big, bn=bn_in)
        attn = _attention(l, proj, attn_sinks, seq=seq, attn_width=attn_width, kv_width=kv_width)
        merged = _merge(l, attn, proj, conv_w, w_ba_b, w_bc_b, seq=seq, d_model=d_model, attn_width=attn_width,
                        conv_width=conv_width, cb_col=cb_col, bm=bm_ln, bn=1024)
        x1 = _oproj_ln(l, merged, w_o_b, xs, g1, b1, alpha=alpha, bm=bm_ln, bn=1024)
        hidden = _up(l, x1, w_up_b, bm=bm_big, bn=1024)
        return _down_ln(l, hidden, w_down_b, x1, g2, b2, alpha=alpha, bm=bm_ln, bk=1024)

    xs = lax.fori_loop(0, depth, layer, x.reshape(m, d_model).astype(BF16))
    return xs.astype(F32).reshape(batch, seq, d_model)
```

```python
import functools
import math

import jax
import jax.numpy as jnp
from jax import lax
from jax.experimental import pallas as pl
from jax.experimental.pallas import tpu as pltpu

HEAD_DIM = 64
KV_RATIO = 8
BLOCK = 128
ROPE_THETA = 500000.0
ROT_DIM = HEAD_DIM // 4
CONV_K = 3
LN_EPS = 1e-5
NEG_INF = -1e30
LOG2E = math.log2(math.e)
LANES = 128
SUBLANES = 8
MXU_WIDTH = 256
CHUNK = 2 * MXU_WIDTH
VMEM_LIMIT_BYTES = 56 * 1024 * 1024

BF16 = jnp.bfloat16
F32 = jnp.float32


def _params(n_axes):
    return pltpu.CompilerParams(dimension_semantics=("arbitrary",) * n_axes, vmem_limit_bytes=VMEM_LIMIT_BYTES)


def _resident(block_shape, index_map):
    return pl.BlockSpec(block_shape, index_map, pipeline_mode=pl.Buffered(1))


def _dot(a, b):
    return jnp.dot(a, b, preferred_element_type=F32)


def _qkv_kernel(l_ref, x_ref, w_ref, c_ref, s1_ref, s2_ref, o_ref, *, attn_width, kv_width):
    x = x_ref[...]
    half = ROT_DIM // 2

    def rope(a):
        return (a * c_ref[...] + pltpu.roll(a, half, 1) * s1_ref[...]
                + pltpu.roll(a, LANES - half, 1) * s2_ref[...])

    q_scale = HEAD_DIM ** -0.5 * LOG2E
    width = o_ref.shape[1]
    for c in range(width // CHUNK):
        acc = _dot(x, w_ref[:, c * CHUNK:(c + 1) * CHUNK])
        for s in range(CHUNK // LANES):
            col = c * CHUNK + s * LANES
            a = acc[:, s * LANES:(s + 1) * LANES]
            if col < attn_width:
                a = rope(a) * q_scale
            elif col < attn_width + kv_width:
                a = rope(a)
            o_ref[:, col:col + LANES] = a.astype(o_ref.dtype)


def _qkv(l, x, w_qkv, rope_c, rope_s1, rope_s2, *, attn_width, kv_width, bm):
    m, d = x.shape
    width = w_qkv.shape[2]
    assert width % CHUNK == 0 and attn_width % LANES == 0 and kv_width % LANES == 0
    row = lambda i, l: (i, 0)
    return pl.pallas_call(
        functools.partial(_qkv_kernel, attn_width=attn_width, kv_width=kv_width),
        grid_spec=pltpu.PrefetchScalarGridSpec(
            num_scalar_prefetch=1,
            grid=(m // bm,),
            in_specs=[
                pl.BlockSpec((bm, d), row),
                _resident((None, d, width), lambda i, l: (l[0], 0, 0)),
                pl.BlockSpec((bm, LANES), row),
                pl.BlockSpec((bm, LANES), row),
                pl.BlockSpec((bm, LANES), row),
            ],
            out_specs=pl.BlockSpec((bm, width), row),
        ),
        out_shape=jax.ShapeDtypeStruct((m, width), BF16),
        compiler_params=_params(1),
        name="qkv",
    )(l, x, w_qkv, rope_c, rope_s1, rope_s2)


def _conv_kernel(l_ref, x_ref, w_ref, cw_ref, o_ref, halo_ref, *, tiles_per_seq):
    i = pl.program_id(0)
    bm, width = o_ref.shape

    @pl.when(i % tiles_per_seq == 0)
    def _():
        halo_ref[...] = jnp.zeros_like(halo_ref)

    x = x_ref[...]
    cw = cw_ref[...]
    row = lax.broadcasted_iota(jnp.int32, (bm, CHUNK), 0)
    for c in range(width // CHUNK):
        sl = slice(c * CHUNK, (c + 1) * CHUNK)
        gate_b = _dot(x, w_ref[:, sl])
        u = _dot(x, w_ref[:, width + c * CHUNK:width + (c + 1) * CHUNK]) * \
            _dot(x, w_ref[:, 2 * width + c * CHUNK:2 * width + (c + 1) * CHUNK])
        prev = halo_ref[:, sl]
        halo_ref[:, sl] = u[bm - SUBLANES:, :]
        conv = cw[CONV_K - 1:CONV_K, sl] * u
        for back in range(1, CONV_K):
            shifted = pltpu.roll(u, back, 0)
            for t in range(back):
                src = SUBLANES - back + t
                shifted = jnp.where(row == t, prev[src:src + 1, :], shifted)
            conv = conv + cw[CONV_K - 1 - back:CONV_K - back, sl] * shifted
        o_ref[:, sl] = (gate_b * conv).astype(o_ref.dtype)


def _short_conv(l, x, w_conv, conv_w, *, seq, bm):
    m, d = x.shape
    width = conv_w.shape[2]
    assert seq % bm == 0 and width % CHUNK == 0 and w_conv.shape[2] == 3 * width
    return pl.pallas_call(
        functools.partial(_conv_kernel, tiles_per_seq=seq // bm),
        grid_spec=pltpu.PrefetchScalarGridSpec(
            num_scalar_prefetch=1,
            grid=(m // bm,),
            in_specs=[
                pl.BlockSpec((bm, d), lambda i, l: (i, 0)),
                _resident((None, d, 3 * width), lambda i, l: (l[0], 0, 0)),
                _resident((None, CONV_K, width), lambda i, l: (l[0], 0, 0)),
            ],
            out_specs=pl.BlockSpec((bm, width), lambda i, l: (i, 0)),
            scratch_shapes=[pltpu.VMEM((SUBLANES, width), F32)],
        ),
        out_shape=jax.ShapeDtypeStruct((m, width), BF16),
        compiler_params=_params(1),
        name="short_conv",
    )(l, x, w_conv, conv_w)


def _mm_kernel(l_ref, x_ref, w_ref, o_ref, *, epilogue):
    x = x_ref[...]
    for c in range(o_ref.shape[1] // CHUNK):
        sl = slice(c * CHUNK, (c + 1) * CHUNK)
        o_ref[:, sl] = epilogue(_dot(x, w_ref[:, sl])).astype(o_ref.dtype)


def _relu_sq(a):
    r = jnp.maximum(a, 0.0)
    return r * r


def _matmul(l, x, w, epilogue, *, bm, bn, name):
    m, d = x.shape
    n = w.shape[2]
    assert bn % CHUNK == 0
    return pl.pallas_call(
        functools.partial(_mm_kernel, epilogue=epilogue),
        grid_spec=pltpu.PrefetchScalarGridSpec(
            num_scalar_prefetch=1,
            grid=(m // bm, n // bn),
            in_specs=[
                pl.BlockSpec((bm, d), lambda i, j, l: (i, 0)),
                pl.BlockSpec((None, d, bn), lambda i, j, l: (l[0], 0, j)),
            ],
            out_specs=pl.BlockSpec((bm, bn), lambda i, j, l: (i, j)),
        ),
        out_shape=jax.ShapeDtypeStruct((m, n), BF16),
        compiler_params=_params(2),
        name=name,
    )(l, x, w)


def _attn_kernel(l_ref, q_ref, kvo_ref, kvp_ref, sink_ref, o_ref, *, nblk, n_kv_heads):
    r = pl.program_id(0)
    n = r % nblk
    layer = l_ref[0]
    kv_w = n_kv_heads * HEAD_DIM
    n_keys = 2 * BLOCK
    kv = jnp.concatenate([kvp_ref[...], kvo_ref[...]], axis=0).astype(F32)

    kj = lax.broadcasted_iota(jnp.int32, (n_keys, BLOCK), 0)
    qi = lax.broadcasted_iota(jnp.int32, (n_keys, BLOCK), 1)
    rel = qi + BLOCK - kj
    mask = (rel >= 0) & (rel < BLOCK) & ((n > 0) | (kj >= BLOCK))
    low_lane = lax.broadcasted_iota(jnp.int32, (n_keys, LANES), 1) < HEAD_DIM
    low_row = lax.broadcasted_iota(jnp.int32, (LANES, n_keys), 0) < HEAD_DIM

    def place(t, low, src_half, dst_half, axis):
        moved = t if src_half == dst_half else pltpu.roll(t, HEAD_DIM, axis)
        keep = jnp.logical_not(low) if dst_half else low
        return jnp.where(keep, moved, 0.0).astype(BF16)

    for kvh in range(n_kv_heads):
        src = kvh % 2
        ks = kv[:, (kvh // 2) * LANES:(kvh // 2 + 1) * LANES]
        vs_t = kv[:, kv_w + (kvh // 2) * LANES:kv_w + (kvh // 2 + 1) * LANES].T
        kz = [place(ks, low_lane, src, p, 1) for p in range(2)]
        vz_t = [place(vs_t, low_row, src, p, 0) for p in range(2)]
        for pair in range(KV_RATIO // 2):
            slab = kvh * (KV_RATIO // 2) + pair
            qs = q_ref[:, slab * LANES:(slab + 1) * LANES]
            out_t = None
            for p in range(2):
                sink = sink_ref[layer, 2 * slab + p] * LOG2E
                s_t = lax.dot_general(kz[p], qs, (((1,), (1,)), ((), ())), preferred_element_type=F32)
                s_t = jnp.where(mask, s_t, NEG_INF)
                mx = jnp.maximum(jnp.max(s_t, axis=0, keepdims=True), sink)
                e = jnp.exp2(s_t - mx)
                den = jnp.sum(e, axis=0, keepdims=True) + jnp.exp2(sink - mx)
                pv_t = _dot(vz_t[p], e.astype(BF16))
                term = pv_t * (1.0 / den)
                out_t = term if out_t is None else out_t + term
            o_ref[:, slab * LANES:(slab + 1) * LANES] = out_t.T.astype(o_ref.dtype)


def _attention(l, qkv, sinks, *, seq, attn_width, kv_width):
    m = qkv.shape[0]
    nblk = seq // BLOCK
    n_kv_heads = kv_width // HEAD_DIM
    assert n_kv_heads % 2 == 0 and attn_width == n_kv_heads * KV_RATIO * HEAD_DIM
    assert attn_width % (2 * kv_width) == 0
    kv_col = attn_width // (2 * kv_width)
    return pl.pallas_call(
        functools.partial(_attn_kernel, nblk=nblk, n_kv_heads=n_kv_heads),
        grid_spec=pltpu.PrefetchScalarGridSpec(
            num_scalar_prefetch=1,
            grid=(m // BLOCK,),
            in_specs=[
                pl.BlockSpec((BLOCK, attn_width), lambda r, l: (r, 0)),
                pl.BlockSpec((BLOCK, 2 * kv_width), lambda r, l: (r, kv_col)),
                pl.BlockSpec((BLOCK, 2 * kv_width), lambda r, l: (jnp.maximum(r - 1, 0), kv_col)),
                pl.BlockSpec(memory_space=pltpu.SMEM),
            ],
            out_specs=pl.BlockSpec((BLOCK, attn_width), lambda r, l: (r, 0)),
        ),
        out_shape=jax.ShapeDtypeStruct((m, attn_width), BF16),
        compiler_params=_params(1),
        name="attn",
    )(l, qkv, qkv, qkv, sinks)


def _merge_kernel(l_ref, attn_ref, conv_ref, wa_ref, wc_ref, ga_ref, gc_ref, o_ref):
    a_in = attn_ref[...]
    c_in = conv_ref[...]
    for c in range(o_ref.shape[1] // CHUNK):
        sl = slice(c * CHUNK, (c + 1) * CHUNK)
        a = _dot(a_in, wa_ref[:, sl])
        cv = _dot(c_in, wc_ref[:, sl])
        o_ref[:, sl] = (ga_ref[:, sl].astype(F32) * a + gc_ref[:, sl].astype(F32) * cv).astype(o_ref.dtype)


def _merge(l, attn, conv, gates, w_br_attn, w_br_conv, *, bm):
    m, attn_width = attn.shape
    conv_width = conv.shape[1]
    d = w_br_attn.shape[2]
    assert gates.shape[1] == 2 * d and d % CHUNK == 0
    return pl.pallas_call(
        _merge_kernel,
        grid_spec=pltpu.PrefetchScalarGridSpec(
            num_scalar_prefetch=1,
            grid=(m // bm,),
            in_specs=[
                pl.BlockSpec((bm, attn_width), lambda i, l: (i, 0)),
                pl.BlockSpec((bm, conv_width), lambda i, l: (i, 0)),
                _resident((None, attn_width, d), lambda i, l: (l[0], 0, 0)),
                _resident((None, conv_width, d), lambda i, l: (l[0], 0, 0)),
                pl.BlockSpec((bm, d), lambda i, l: (i, 0)),
                pl.BlockSpec((bm, d), lambda i, l: (i, 1)),
            ],
            out_specs=pl.BlockSpec((bm, d), lambda i, l: (i, 0)),
        ),
        out_shape=jax.ShapeDtypeStruct((m, d), BF16),
        compiler_params=_params(1),
        name="merge",
    )(l, attn, conv, w_br_attn, w_br_conv, gates, gates)


def _layer_norm_store(load_chunk, n_chunks, g_ref, b_ref, o_ref):
    d = o_ref.shape[1]
    w = d // n_chunks
    total = load_chunk(0).sum(axis=1, keepdims=True)
    for c in range(1, n_chunks):
        total = total + load_chunk(c).sum(axis=1, keepdims=True)
    mu = total * (1.0 / d)
    sq = jnp.square(load_chunk(0) - mu).sum(axis=1, keepdims=True)
    for c in range(1, n_chunks):
        sq = sq + jnp.square(load_chunk(c) - mu).sum(axis=1, keepdims=True)
    rstd = lax.rsqrt(sq * (1.0 / d) + LN_EPS)
    for c in range(n_chunks):
        sl = slice(c * w, (c + 1) * w)
        o_ref[:, sl] = ((load_chunk(c) - mu) * rstd * g_ref[:, sl] + b_ref[:, sl]).astype(o_ref.dtype)


def _oproj_kernel(l_ref, a_ref, w_ref, xres_ref, g_ref, b_ref, o_ref, y_ref, *, alpha, n_tiles):
    j = pl.program_id(1)
    a_in = a_ref[...]
    bn = w_ref.shape[1]
    for c in range(bn // CHUNK):
        sl = slice(c * CHUNK, (c + 1) * CHUNK)
        y_ref[j, :, sl] = alpha * xres_ref[:, sl].astype(F32) + _dot(a_in, w_ref[:, sl])

    @pl.when(j == n_tiles - 1)
    def _():
        _layer_norm_store(lambda c: y_ref[c], n_tiles, g_ref, b_ref, o_ref)


def _oproj_ln(l, merged, w_o, xres, g, b, *, alpha, bm, bn):
    m, d = merged.shape
    n_tiles = d // bn
    assert bn % CHUNK == 0
    return pl.pallas_call(
        functools.partial(_oproj_kernel, alpha=alpha, n_tiles=n_tiles),
        grid_spec=pltpu.PrefetchScalarGridSpec(
            num_scalar_prefetch=1,
            grid=(m // bm, n_tiles),
            in_specs=[
                pl.BlockSpec((bm, d), lambda i, j, l: (i, 0)),
                pl.BlockSpec((None, d, bn), lambda i, j, l: (l[0], 0, j)),
                pl.BlockSpec((bm, bn), lambda i, j, l: (i, j)),
                pl.BlockSpec((None, 1, d), lambda i, j, l: (l[0], 0, 0)),
                pl.BlockSpec((None, 1, d), lambda i, j, l: (l[0], 0, 0)),
            ],
            out_specs=pl.BlockSpec((bm, d), lambda i, j, l: (i, 0)),
            scratch_shapes=[pltpu.VMEM((n_tiles, bm, bn), F32)],
        ),
        out_shape=jax.ShapeDtypeStruct((m, d), BF16),
        compiler_params=_params(2),
        name="oproj_ln",
    )(l, merged, w_o, xres, g, b)


def _down_kernel(l_ref, h_ref, w_ref, xres_ref, g_ref, b_ref, o_ref, acc_ref, *, alpha, n_k):
    k = pl.program_id(1)
    n_chunks = acc_ref.shape[1] // CHUNK

    @pl.when(k == 0)
    def _():
        acc_ref[...] = alpha * xres_ref[...].astype(F32)

    h = h_ref[...]
    for c in range(n_chunks):
        sl = slice(c * CHUNK, (c + 1) * CHUNK)
        acc_ref[:, sl] += _dot(h, w_ref[:, sl])

    @pl.when(k == n_k - 1)
    def _():
        _layer_norm_store(lambda c: acc_ref[:, c * CHUNK:(c + 1) * CHUNK], n_chunks, g_ref, b_ref, o_ref)


def _down_ln(l, hidden, w_down, xres, g, b, *, alpha, bm, bk):
    m, d_ff = hidden.shape
    d = xres.shape[1]
    n_k = d_ff // bk
    assert d % CHUNK == 0
    return pl.pallas_call(
        functools.partial(_down_kernel, alpha=alpha, n_k=n_k),
        grid_spec=pltpu.PrefetchScalarGridSpec(
            num_scalar_prefetch=1,
            grid=(m // bm, n_k),
            in_specs=[
                pl.BlockSpec((bm, bk), lambda i, k, l: (i, k)),
                pl.BlockSpec((None, bk, d), lambda i, k, l: (l[0], k, 0)),
                pl.BlockSpec((bm, d), lambda i, k, l: (i, 0)),
                pl.BlockSpec((None, 1, d), lambda i, k, l: (l[0], 0, 0)),
                pl.BlockSpec((None, 1, d), lambda i, k, l: (l[0], 0, 0)),
            ],
            out_specs=pl.BlockSpec((bm, d), lambda i, k, l: (i, 0)),
            scratch_shapes=[pltpu.VMEM((bm, d), F32)],
        ),
        out_shape=jax.ShapeDtypeStruct((m, d), BF16),
        compiler_params=_params(2),
        name="down_ln",
    )(l, hidden, w_down, xres, g, b)


def _rope_lane_tables(positions):
    half = ROT_DIM // 2
    inv_freq = ROPE_THETA ** (-jnp.arange(0, ROT_DIM, 2, dtype=F32) / ROT_DIM)
    ang = positions.astype(F32).reshape(-1, 1) * inv_freq
    cos, sin = jnp.cos(ang), jnp.sin(ang)
    m = ang.shape[0]
    zeros = lambda w: jnp.zeros((m, w), F32)
    c64 = jnp.concatenate([cos, cos, jnp.ones((m, HEAD_DIM - ROT_DIM), F32)], axis=1)
    s1_64 = jnp.concatenate([zeros(half), sin, zeros(HEAD_DIM - ROT_DIM)], axis=1)
    s2_64 = jnp.concatenate([-sin, zeros(HEAD_DIM - half)], axis=1)
    rep = LANES // HEAD_DIM
    return jnp.tile(c64, (1, rep)), jnp.tile(s1_64, (1, rep)), jnp.tile(s2_64, (1, rep))


def kernel(x, positions, w_in, conv_w, attn_sinks, w_br_attn, w_br_conv, w_o, ln1_g, ln1_b, w_up, w_down, ln2_g, ln2_b):
    batch, seq, d_model = x.shape
    depth = w_in.shape[0]
    m = batch * seq
    attn_width = w_br_attn.shape[1]
    conv_width = w_br_conv.shape[1]
    kv_width = attn_width // KV_RATIO
    qkv_width = attn_width + 2 * kv_width
    assert w_in.shape[2] == qkv_width + 3 * conv_width + 2 * d_model
    alpha = (2 * depth) ** 0.25
    bm_big = min(1024, seq)
    bm_mid = min(512, seq)
    bm_small = min(256, seq)

    rope_c, rope_s1, rope_s2 = _rope_lane_tables(positions)
    w_in_b = w_in.astype(BF16)
    w_qkv = w_in_b[:, :, :qkv_width]
    w_conv = w_in_b[:, :, qkv_width:qkv_width + 3 * conv_width]
    w_gate = w_in_b[:, :, qkv_width + 3 * conv_width:]
    w_ba_b, w_bc_b, w_o_b, w_up_b, w_down_b = (w.astype(BF16) for w in (w_br_attn, w_br_conv, w_o, w_up, w_down))
    g1, b1, g2, b2 = (t.reshape(depth, 1, d_model) for t in (ln1_g, ln1_b, ln2_g, ln2_b))

    def layer(li, xs):
        l = jnp.reshape(li, (1,)).astype(jnp.int32)
        qkv = _qkv(l, xs, w_qkv, rope_c, rope_s1, rope_s2, attn_width=attn_width, kv_width=kv_width, bm=bm_mid)
        conv = _short_conv(l, xs, w_conv, conv_w, seq=seq, bm=bm_mid)
        gates = _matmul(l, xs, w_gate, jax.nn.sigmoid, bm=bm_big, bn=1024, name="gates")
        attn = _attention(l, qkv, attn_sinks, seq=seq, attn_width=attn_width, kv_width=kv_width)
        merged = _merge(l, attn, conv, gates, w_ba_b, w_bc_b, bm=bm_small)
        x1 = _oproj_ln(l, merged, w_o_b, xs, g1, b1, alpha=alpha, bm=bm_mid, bn=1024)
        hidden = _matmul(l, x1, w_up_b, _relu_sq, bm=bm_big, bn=1024, name="up")
        return _down_ln(l, hidden, w_down_b, x1, g2, b2, alpha=alpha, bm=bm_mid, bk=1024)

    xs = lax.fori_loop(0, depth, layer, x.reshape(m, d_model).astype(BF16))
    return xs.astype(F32).reshape(batch, seq, d_model)
```

```python
import functools
import math

import jax
import jax.numpy as jnp
from jax import lax
from jax.experimental import pallas as pl
from jax.experimental.pallas import tpu as pltpu

HEAD_DIM = 64
KV_RATIO = 8
BLOCK = 128
ROPE_THETA = 500000.0
ROT_DIM = HEAD_DIM // 4
CONV_K = 3
LN_EPS = 1e-5
NEG_INF = -1e30
LOG2E = math.log2(math.e)
LANES = 128
SUBLANES = 8
BF16_SUBLANES = 16
MXU_WIDTH = 256
CHUNK = 2 * MXU_WIDTH
VMEM_LIMIT_BYTES = 60 * 1024 * 1024

BF16 = jnp.bfloat16
F32 = jnp.float32


def _params(n_axes):
    return pltpu.CompilerParams(dimension_semantics=("arbitrary",) * n_axes, vmem_limit_bytes=VMEM_LIMIT_BYTES)


def _resident(block_shape, index_map):
    return pl.BlockSpec(block_shape, index_map, pipeline_mode=pl.Buffered(1))


def _dot(a, b):
    return jnp.dot(a, b, preferred_element_type=F32)


class _Cast:
    def __init__(self, w, n_steps, splits=None, layer_of=lambda l: l[0]):
        self.w = w
        _, rows, cols = w.shape
        assert rows % n_steps == 0 and (rows // n_steps) % BF16_SUBLANES == 0
        self.rows, self.cols, self.blk = rows, cols, rows // n_steps
        self.splits = splits or (cols,)
        assert sum(self.splits) == cols and all(s % LANES == 0 for s in self.splits)
        self.layer_of = layer_of

    def in_spec(self, step_of):
        return pl.BlockSpec((None, self.blk, self.cols), lambda *a: (self.layer_of(a[-1]), step_of(*a[:-1]), 0))

    def out_specs(self, step_of):
        return [pl.BlockSpec((self.blk, s), lambda *a: (step_of(*a[:-1]), 0)) for s in self.splits]

    def out_shapes(self):
        return [jax.ShapeDtypeStruct((self.rows, s), BF16) for s in self.splits]


def _run_casts(casts, in_refs, out_refs):
    o = 0
    for cast, src in zip(casts, in_refs):
        col = 0
        for s in cast.splits:
            out_refs[o][...] = src[:, col:col + s].astype(BF16)
            col += s
            o += 1


def _call_with_casts(body, n_main_in, casts, step_of, *, grid, in_specs, out_spec, out_shape, scratch_shapes=(),
                     name, operands):
    n_cast_out = sum(len(c.splits) for c in casts)

    def kernel_fn(l_ref, *refs):
        main_in = refs[:n_main_in]
        cast_in = refs[n_main_in:n_main_in + len(casts)]
        main_out = refs[n_main_in + len(casts)]
        cast_out = refs[n_main_in + len(casts) + 1:n_main_in + len(casts) + 1 + n_cast_out]
        scratch = refs[n_main_in + len(casts) + 1 + n_cast_out:]
        _run_casts(casts, cast_in, cast_out)
        body(l_ref, *main_in, main_out, *scratch)

    outs = pl.pallas_call(
        kernel_fn,
        grid_spec=pltpu.PrefetchScalarGridSpec(
            num_scalar_prefetch=1,
            grid=grid,
            in_specs=list(in_specs) + [c.in_spec(step_of) for c in casts],
            out_specs=[out_spec] + [s for c in casts for s in c.out_specs(step_of)],
            scratch_shapes=list(scratch_shapes),
        ),
        out_shape=[out_shape] + [s for c in casts for s in c.out_shapes()],
        compiler_params=_params(len(grid)),
        name=name,
    )(*operands, *[c.w for c in casts])
    return outs


def _qkv_body(l_ref, x_ref, w_ref, c_ref, s1_ref, s2_ref, o_ref, *, attn_width, kv_width):
    x = x_ref[...]
    half = ROT_DIM // 2

    def rope(a):
        return (a * c_ref[...] + pltpu.roll(a, half, 1) * s1_ref[...]
                + pltpu.roll(a, LANES - half, 1) * s2_ref[...])

    q_scale = HEAD_DIM ** -0.5 * LOG2E
    width = o_ref.shape[1]
    for c in range(width // CHUNK):
        acc = _dot(x, w_ref[:, c * CHUNK:(c + 1) * CHUNK])
        for s in range(CHUNK // LANES):
            col = c * CHUNK + s * LANES
            a = acc[:, s * LANES:(s + 1) * LANES]
            if col < attn_width:
                a = rope(a) * q_scale
            elif col < attn_width + kv_width:
                a = rope(a)
            o_ref[:, col:col + LANES] = a.astype(o_ref.dtype)


def _qkv(l, x, w_qkv, rope_c, rope_s1, rope_s2, casts, *, attn_width, kv_width, bm):
    m, d = x.shape
    width = w_qkv.shape[1]
    assert width % CHUNK == 0 and attn_width % LANES == 0 and kv_width % LANES == 0
    row = lambda i, l: (i, 0)
    return _call_with_casts(
        functools.partial(_qkv_body, attn_width=attn_width, kv_width=kv_width), 5, casts, lambda i: i,
        grid=(m // bm,),
        in_specs=[pl.BlockSpec((bm, d), row), _resident((d, width), lambda i, l: (0, 0)),
                  pl.BlockSpec((bm, LANES), row), pl.BlockSpec((bm, LANES), row), pl.BlockSpec((bm, LANES), row)],
        out_spec=pl.BlockSpec((bm, width), row),
        out_shape=jax.ShapeDtypeStruct((m, width), BF16),
        name="qkv", operands=(l, x, w_qkv, rope_c, rope_s1, rope_s2))


def _conv_body(l_ref, x_ref, w_ref, cw_ref, o_ref, halo_ref, *, tiles_per_seq):
    i = pl.program_id(0)
    bm, width = o_ref.shape

    @pl.when(i % tiles_per_seq == 0)
    def _():
        halo_ref[...] = jnp.zeros_like(halo_ref)

    x = x_ref[...]
    cw = cw_ref[...]
    row = lax.broadcasted_iota(jnp.int32, (bm, CHUNK), 0)
    for c in range(width // CHUNK):
        sl = slice(c * CHUNK, (c + 1) * CHUNK)
        gate_b = _dot(x, w_ref[:, sl])
        u = _dot(x, w_ref[:, width + c * CHUNK:width + (c + 1) * CHUNK]) * \
            _dot(x, w_ref[:, 2 * width + c * CHUNK:2 * width + (c + 1) * CHUNK])
        prev = halo_ref[:, sl]
        halo_ref[:, sl] = u[bm - SUBLANES:, :]
        conv = cw[CONV_K - 1:CONV_K, sl] * u
        for back in range(1, CONV_K):
            shifted = pltpu.roll(u, back, 0)
            for t in range(back):
                src = SUBLANES - back + t
                shifted = jnp.where(row == t, prev[src:src + 1, :], shifted)
            conv = conv + cw[CONV_K - 1 - back:CONV_K - back, sl] * shifted
        o_ref[:, sl] = (gate_b * conv).astype(o_ref.dtype)


def _short_conv(l, x, w_conv, conv_w, casts, *, seq, bm):
    m, d = x.shape
    width = conv_w.shape[2]
    assert seq % bm == 0 and width % CHUNK == 0 and w_conv.shape[1] == 3 * width
    return _call_with_casts(
        functools.partial(_conv_body, tiles_per_seq=seq // bm), 3, casts, lambda i: i,
        grid=(m // bm,),
        in_specs=[pl.BlockSpec((bm, d), lambda i, l: (i, 0)), _resident((d, 3 * width), lambda i, l: (0, 0)),
                  _resident((None, CONV_K, width), lambda i, l: (l[0], 0, 0))],
        out_spec=pl.BlockSpec((bm, width), lambda i, l: (i, 0)),
        out_shape=jax.ShapeDtypeStruct((m, width), BF16),
        scratch_shapes=[pltpu.VMEM((SUBLANES, width), F32)],
        name="short_conv", operands=(l, x, w_conv, conv_w))


def _mm_body(l_ref, x_ref, w_ref, o_ref, *, epilogue):
    x = x_ref[...]
    for c in range(o_ref.shape[1] // CHUNK):
        sl = slice(c * CHUNK, (c + 1) * CHUNK)
        o_ref[:, sl] = epilogue(_dot(x, w_ref[:, sl])).astype(o_ref.dtype)


def _relu_sq(a):
    r = jnp.maximum(a, 0.0)
    return r * r


def _matmul(l, x, w, epilogue, casts, *, bm, bn, name):
    m, d = x.shape
    n = w.shape[1]
    assert bn % CHUNK == 0
    n_j = n // bn
    return _call_with_casts(
        functools.partial(_mm_body, epilogue=epilogue), 2, casts, lambda i, j: i * n_j + j,
        grid=(m // bm, n_j),
        in_specs=[pl.BlockSpec((bm, d), lambda i, j, l: (i, 0)), pl.BlockSpec((d, bn), lambda i, j, l: (0, j))],
        out_spec=pl.BlockSpec((bm, bn), lambda i, j, l: (i, j)),
        out_shape=jax.ShapeDtypeStruct((m, n), BF16),
        name=name, operands=(l, x, w))


def _attn_kernel(l_ref, q_ref, kvo_ref, kvp_ref, sink_ref, o_ref, *, nblk, n_kv_heads):
    r = pl.program_id(0)
    n = r % nblk
    layer = l_ref[0]
    kv_w = n_kv_heads * HEAD_DIM
    n_keys = 2 * BLOCK
    kv = jnp.concatenate([kvp_ref[...], kvo_ref[...]], axis=0).astype(F32)

    kj = lax.broadcasted_iota(jnp.int32, (n_keys, BLOCK), 0)
    qi = lax.broadcasted_iota(jnp.int32, (n_keys, BLOCK), 1)
    rel = qi + BLOCK - kj
    mask = (rel >= 0) & (rel < BLOCK) & ((n > 0) | (kj >= BLOCK))
    low_lane = lax.broadcasted_iota(jnp.int32, (n_keys, LANES), 1) < HEAD_DIM
    low_row = lax.broadcasted_iota(jnp.int32, (LANES, n_keys), 0) < HEAD_DIM

    def place(t, low, src_half, dst_half, axis):
        moved = t if src_half == dst_half else pltpu.roll(t, HEAD_DIM, axis)
        keep = jnp.logical_not(low) if dst_half else low
        return jnp.where(keep, moved, 0.0).astype(BF16)

    for kvh in range(n_kv_heads):
        src = kvh % 2
        ks = kv[:, (kvh // 2) * LANES:(kvh // 2 + 1) * LANES]
        vs_t = kv[:, kv_w + (kvh // 2) * LANES:kv_w + (kvh // 2 + 1) * LANES].T
        kz = [place(ks, low_lane, src, p, 1) for p in range(2)]
        vz_t = [place(vs_t, low_row, src, p, 0) for p in range(2)]
        for pair in range(KV_RATIO // 2):
            slab = kvh * (KV_RATIO // 2) + pair
            qs = q_ref[:, slab * LANES:(slab + 1) * LANES]
            out_t = None
            for p in range(2):
                sink = sink_ref[layer, 2 * slab + p] * LOG2E
                s_t = lax.dot_general(kz[p], qs, (((1,), (1,)), ((), ())), preferred_element_type=F32)
                s_t = jnp.where(mask, s_t, NEG_INF)
                mx = jnp.maximum(jnp.max(s_t, axis=0, keepdims=True), sink)
                e = jnp.exp2(s_t - mx)
                den = jnp.sum(e, axis=0, keepdims=True) + jnp.exp2(sink - mx)
                pv_t = _dot(vz_t[p], e.astype(BF16))
                term = pv_t * (1.0 / den)
                out_t = term if out_t is None else out_t + term
            o_ref[:, slab * LANES:(slab + 1) * LANES] = out_t.T.astype(o_ref.dtype)


def _attention(l, qkv, sinks, *, seq, attn_width, kv_width):
    m = qkv.shape[0]
    nblk = seq // BLOCK
    n_kv_heads = kv_width // HEAD_DIM
    assert n_kv_heads % 2 == 0 and attn_width == n_kv_heads * KV_RATIO * HEAD_DIM
    assert attn_width % (2 * kv_width) == 0
    kv_col = attn_width // (2 * kv_width)
    return pl.pallas_call(
        functools.partial(_attn_kernel, nblk=nblk, n_kv_heads=n_kv_heads),
        grid_spec=pltpu.PrefetchScalarGridSpec(
            num_scalar_prefetch=1,
            grid=(m // BLOCK,),
            in_specs=[
                pl.BlockSpec((BLOCK, attn_width), lambda r, l: (r, 0)),
                pl.BlockSpec((BLOCK, 2 * kv_width), lambda r, l: (r, kv_col)),
                pl.BlockSpec((BLOCK, 2 * kv_width), lambda r, l: (jnp.maximum(r - 1, 0), kv_col)),
                pl.BlockSpec(memory_space=pltpu.SMEM),
            ],
            out_specs=pl.BlockSpec((BLOCK, attn_width), lambda r, l: (r, 0)),
        ),
        out_shape=jax.ShapeDtypeStruct((m, attn_width), BF16),
        compiler_params=_params(1),
        name="attn",
    )(l, qkv, qkv, qkv, sinks)


def _merge_kernel(attn_ref, conv_ref, wa_ref, wc_ref, ga_ref, gc_ref, o_ref):
    a_in = attn_ref[...]
    c_in = conv_ref[...]
    for c in range(o_ref.shape[1] // CHUNK):
        sl = slice(c * CHUNK, (c + 1) * CHUNK)
        a = _dot(a_in, wa_ref[:, sl])
        cv = _dot(c_in, wc_ref[:, sl])
        o_ref[:, sl] = (ga_ref[:, sl].astype(F32) * a + gc_ref[:, sl].astype(F32) * cv).astype(o_ref.dtype)


def _merge(attn, conv, gates, w_br_attn, w_br_conv, *, bm):
    m, attn_width = attn.shape
    conv_width = conv.shape[1]
    d = w_br_attn.shape[1]
    assert gates.shape[1] == 2 * d and d % CHUNK == 0
    return pl.pallas_call(
        _merge_kernel,
        grid=(m // bm,),
        in_specs=[
            pl.BlockSpec((bm, attn_width), lambda i: (i, 0)),
            pl.BlockSpec((bm, conv_width), lambda i: (i, 0)),
            _resident((attn_width, d), lambda i: (0, 0)),
            _resident((conv_width, d), lambda i: (0, 0)),
            pl.BlockSpec((bm, d), lambda i: (i, 0)),
            pl.BlockSpec((bm, d), lambda i: (i, 1)),
        ],
        out_specs=pl.BlockSpec((bm, d), lambda i: (i, 0)),
        out_shape=jax.ShapeDtypeStruct((m, d), BF16),
        compiler_params=_params(1),
        name="merge",
    )(attn, conv, w_br_attn, w_br_conv, gates, gates)


def _layer_norm_store(load_chunk, n_chunks, g_ref, b_ref, o_ref):
    d = o_ref.shape[1]
    w = d // n_chunks
    means, m2s = [], []
    for c in range(n_chunks):
        y = load_chunk(c)
        mean_c = y.sum(axis=1, keepdims=True) * (1.0 / w)
        means.append(mean_c)
        m2s.append(jnp.square(y - mean_c).sum(axis=1, keepdims=True))
    mu = sum(means[1:], means[0]) * (1.0 / n_chunks)
    m2 = sum(m2s[1:], m2s[0]) + w * sum((jnp.square(mc - mu) for mc in means[1:]), jnp.square(means[0] - mu))
    rstd = lax.rsqrt(m2 * (1.0 / d) + LN_EPS)
    for c in range(n_chunks):
        sl = slice(c * w, (c + 1) * w)
        o_ref[:, sl] = ((load_chunk(c) - mu) * rstd * g_ref[:, sl] + b_ref[:, sl]).astype(o_ref.dtype)


def _oproj_kernel(l_ref, a_ref, w_ref, xres_ref, g_ref, b_ref, o_ref, y0_ref, y1_ref, *, alpha, n_tiles):
    i = pl.program_id(0)
    y_refs = (y0_ref, y1_ref)
    n_chunks = y0_ref.shape[0]

    def project(y_ref):
        a_in = a_ref[...]
        for c in range(n_chunks):
            sl = slice(c * CHUNK, (c + 1) * CHUNK)
            y_ref[c] = alpha * xres_ref[:, sl].astype(F32) + _dot(a_in, w_ref[:, sl])

    def normalize(y_ref):
        _layer_norm_store(lambda c: y_ref[c], n_chunks, g_ref, b_ref, o_ref)

    @pl.when(i == 0)
    def _():
        project(y_refs[0])

    for parity in range(2):
        @pl.when((i > 0) & (i < n_tiles) & (i % 2 == parity))
        def _():
            normalize(y_refs[1 - parity])
            project(y_refs[parity])

    @pl.when(i == n_tiles)
    def _():
        normalize(y_refs[(n_tiles - 1) % 2])


def _oproj_ln(l, merged, w_o, xres, g, b, *, alpha, bm):
    m, d = merged.shape
    n_tiles = m // bm
    assert d % CHUNK == 0
    cur = lambda i, l: (jnp.minimum(i, n_tiles - 1), 0)
    return pl.pallas_call(
        functools.partial(_oproj_kernel, alpha=alpha, n_tiles=n_tiles),
        grid_spec=pltpu.PrefetchScalarGridSpec(
            num_scalar_prefetch=1,
            grid=(n_tiles + 1,),
            in_specs=[
                pl.BlockSpec((bm, d), cur),
                _resident((d, d), lambda i, l: (0, 0)),
                pl.BlockSpec((bm, d), cur),
                pl.BlockSpec((None, 1, d), lambda i, l: (l[0], 0, 0)),
                pl.BlockSpec((None, 1, d), lambda i, l: (l[0], 0, 0)),
            ],
            out_specs=pl.BlockSpec((bm, d), lambda i, l: (jnp.maximum(i - 1, 0), 0)),
            scratch_shapes=[pltpu.VMEM((d // CHUNK, bm, CHUNK), F32)] * 2,
        ),
        out_shape=jax.ShapeDtypeStruct((m, d), BF16),
        compiler_params=_params(1),
        name="oproj_ln",
    )(l, merged, w_o, xres, g, b)


def _down_kernel(l_ref, h_ref, w_ref, xres_ref, g_ref, b_ref, o_ref, acc_ref, *, alpha, n_k):
    k = pl.program_id(1)
    n_chunks = acc_ref.shape[0]

    def accumulate(first):
        h = h_ref[...]
        for c in range(n_chunks):
            sl = slice(c * CHUNK, (c + 1) * CHUNK)
            base = alpha * xres_ref[:, sl].astype(F32) if first else acc_ref[c]
            acc_ref[c] = base + _dot(h, w_ref[:, sl])

    @pl.when(k == 0)
    def _():
        accumulate(True)

    @pl.when((k > 0) & (k < n_k - 1))
    def _():
        accumulate(False)

    @pl.when(k == n_k - 1)
    def _():
        accumulate(False)
        _layer_norm_store(lambda c: acc_ref[c], n_chunks, g_ref, b_ref, o_ref)


def _down_ln(l, hidden, w_down, xres, g, b, *, alpha, bm, bk):
    m, d_ff = hidden.shape
    d = xres.shape[1]
    n_k = d_ff // bk
    assert d % CHUNK == 0 and n_k >= 2
    return pl.pallas_call(
        functools.partial(_down_kernel, alpha=alpha, n_k=n_k),
        grid_spec=pltpu.PrefetchScalarGridSpec(
            num_scalar_prefetch=1,
            grid=(m // bm, n_k),
            in_specs=[
                pl.BlockSpec((bm, bk), lambda i, k, l: (i, k)),
                pl.BlockSpec((bk, d), lambda i, k, l: (k, 0)),
                pl.BlockSpec((bm, d), lambda i, k, l: (i, 0)),
                pl.BlockSpec((None, 1, d), lambda i, k, l: (l[0], 0, 0)),
                pl.BlockSpec((None, 1, d), lambda i, k, l: (l[0], 0, 0)),
            ],
            out_specs=pl.BlockSpec((bm, d), lambda i, k, l: (i, 0)),
            scratch_shapes=[pltpu.VMEM((d // CHUNK, bm, CHUNK), F32)],
        ),
        out_shape=jax.ShapeDtypeStruct((m, d), BF16),
        compiler_params=_params(2),
        name="down_ln",
    )(l, hidden, w_down, xres, g, b)


def _rope_lane_tables(positions):
    half = ROT_DIM // 2
    inv_freq = ROPE_THETA ** (-jnp.arange(0, ROT_DIM, 2, dtype=F32) / ROT_DIM)
    ang = positions.astype(F32).reshape(-1, 1) * inv_freq
    cos, sin = jnp.cos(ang), jnp.sin(ang)
    m = ang.shape[0]
    zeros = lambda w: jnp.zeros((m, w), F32)
    c64 = jnp.concatenate([cos, cos, jnp.ones((m, HEAD_DIM - ROT_DIM), F32)], axis=1)
    s1_64 = jnp.concatenate([zeros(half), sin, zeros(HEAD_DIM - ROT_DIM)], axis=1)
    s2_64 = jnp.concatenate([-sin, zeros(HEAD_DIM - half)], axis=1)
    rep = LANES // HEAD_DIM
    return jnp.tile(c64, (1, rep)), jnp.tile(s1_64, (1, rep)), jnp.tile(s2_64, (1, rep))


def kernel(x, positions, w_in, conv_w, attn_sinks, w_br_attn, w_br_conv, w_o, ln1_g, ln1_b, w_up, w_down, ln2_g, ln2_b):
    batch, seq, d_model = x.shape
    depth = w_in.shape[0]
    m = batch * seq
    attn_width = w_br_attn.shape[1]
    conv_width = w_br_conv.shape[1]
    d_ff = w_up.shape[2]
    kv_width = attn_width // KV_RATIO
    qkv_width = attn_width + 2 * kv_width
    in_splits = (qkv_width, 3 * conv_width, 2 * d_model)
    assert w_in.shape[2] == sum(in_splits)
    alpha = (2 * depth) ** 0.25
    bm_big = min(1024, seq)
    bm_mid = min(512, seq)
    bm_small = min(256, seq)
    bn = 1024
    steps_mid = m // bm_mid
    steps_gates = (m // bm_big) * (2 * d_model // bn)
    steps_up = (m // bm_big) * (d_ff // bn)

    rope_c, rope_s1, rope_s2 = _rope_lane_tables(positions)
    g1, b1, g2, b2 = (t.reshape(depth, 1, d_model) for t in (ln1_g, ln1_b, ln2_g, ln2_b))
    w_in0 = tuple(w_in[0, :, a:a + s].astype(BF16)
                  for a, s in zip((0, in_splits[0], in_splits[0] + in_splits[1]), in_splits))
    next_layer = lambda l: jnp.minimum(l[0] + 1, depth - 1)

    def layer(li, carry):
        xs, w_qkv, w_conv, w_gate = carry
        l = jnp.reshape(li, (1,)).astype(jnp.int32)
        qkv, w_ba = _qkv(l, xs, w_qkv, rope_c, rope_s1, rope_s2, [_Cast(w_br_attn, steps_mid)],
                         attn_width=attn_width, kv_width=kv_width, bm=bm_mid)
        conv, w_bc, w_op = _short_conv(l, xs, w_conv, conv_w, [_Cast(w_br_conv, steps_mid), _Cast(w_o, steps_mid)],
                                       seq=seq, bm=bm_mid)
        gates, w_u = _matmul(l, xs, w_gate, jax.nn.sigmoid, [_Cast(w_up, steps_gates)], bm=bm_big, bn=bn,
                             name="gates")
        attn = _attention(l, qkv, attn_sinks, seq=seq, attn_width=attn_width, kv_width=kv_width)
        merged = _merge(attn, conv, gates, w_ba, w_bc, bm=bm_small)
        x1 = _oproj_ln(l, merged, w_op, xs, g1, b1, alpha=alpha, bm=bm_small)
        hidden, w_d, w_qkv_n, w_conv_n, w_gate_n = _matmul(
            l, x1, w_u, _relu_sq, [_Cast(w_down, steps_up), _Cast(w_in, steps_up, in_splits, next_layer)],
            bm=bm_big, bn=bn, name="up")
        x2 = _down_ln(l, hidden, w_d, x1, g2, b2, alpha=alpha, bm=bm_mid, bk=bn)
        return x2, w_qkv_n, w_conv_n, w_gate_n

    xs, *_ = lax.fori_loop(0, depth, layer, (x.reshape(m, d_model).astype(BF16),) + w_in0)
    return xs.astype(F32).reshape(batch, seq, d_model)
```

```python
import functools
import math

import jax
import jax.numpy as jnp
from jax import lax
from jax.experimental import pallas as pl
from jax.experimental.pallas import tpu as pltpu

HEAD_DIM = 64
KV_RATIO = 8
BLOCK = 128
ROPE_THETA = 500000.0
ROT_DIM = HEAD_DIM // 4
CONV_K = 3
LN_EPS = 1e-5
NEG_INF = -1e30
LOG2E = math.log2(math.e)
LANES = 128
SUBLANES = 8
BF16_SUBLANES = 16
MXU_WIDTH = 256
Q_GROUP = 1
CHUNK = 2 * MXU_WIDTH
VMEM_LIMIT_BYTES = 60 * 1024 * 1024

BF16 = jnp.bfloat16
F32 = jnp.float32


def _params(n_axes):
    return pltpu.CompilerParams(dimension_semantics=("arbitrary",) * n_axes, vmem_limit_bytes=VMEM_LIMIT_BYTES)


def _resident(block_shape, index_map):
    return pl.BlockSpec(block_shape, index_map, pipeline_mode=pl.Buffered(1))


def _dot(a, b):
    return jnp.dot(a, b, preferred_element_type=F32)


class _Cast:
    def __init__(self, w, n_steps, splits=None, layer_of=lambda l: l[0]):
        self.w = w
        _, rows, cols = w.shape
        assert rows % n_steps == 0 and (rows // n_steps) % BF16_SUBLANES == 0
        self.rows, self.cols, self.blk = rows, cols, rows // n_steps
        self.splits = splits or (cols,)
        assert sum(self.splits) == cols and all(s % LANES == 0 for s in self.splits)
        self.layer_of = layer_of

    def in_spec(self, step_of):
        return pl.BlockSpec((None, self.blk, self.cols), lambda *a: (self.layer_of(a[-1]), step_of(*a[:-1]), 0))

    def out_specs(self, step_of):
        return [pl.BlockSpec((self.blk, s), lambda *a: (step_of(*a[:-1]), 0)) for s in self.splits]

    def out_shapes(self):
        return [jax.ShapeDtypeStruct((self.rows, s), BF16) for s in self.splits]


def _run_casts(casts, in_refs, out_refs):
    o = 0
    for cast, src in zip(casts, in_refs):
        col = 0
        for s in cast.splits:
            out_refs[o][...] = src[:, col:col + s].astype(BF16)
            col += s
            o += 1


def _call_with_casts(body, n_main_in, casts, step_of, *, grid, in_specs, out_spec, out_shape, scratch_shapes=(),
                     name, operands):
    n_cast_out = sum(len(c.splits) for c in casts)

    def kernel_fn(l_ref, *refs):
        main_in = refs[:n_main_in]
        cast_in = refs[n_main_in:n_main_in + len(casts)]
        main_out = refs[n_main_in + len(casts)]
        cast_out = refs[n_main_in + len(casts) + 1:n_main_in + len(casts) + 1 + n_cast_out]
        scratch = refs[n_main_in + len(casts) + 1 + n_cast_out:]
        _run_casts(casts, cast_in, cast_out)
        body(l_ref, *main_in, main_out, *scratch)

    outs = pl.pallas_call(
        kernel_fn,
        grid_spec=pltpu.PrefetchScalarGridSpec(
            num_scalar_prefetch=1,
            grid=grid,
            in_specs=list(in_specs) + [c.in_spec(step_of) for c in casts],
            out_specs=[out_spec] + [s for c in casts for s in c.out_specs(step_of)],
            scratch_shapes=list(scratch_shapes),
        ),
        out_shape=[out_shape] + [s for c in casts for s in c.out_shapes()],
        compiler_params=_params(len(grid)),
        name=name,
    )(*operands, *[c.w for c in casts])
    return outs


def _cast_only(l, cast, n_steps):
    step_of = lambda i: i
    return pl.pallas_call(
        lambda l_ref, src, *outs: _run_casts([cast], [src], outs),
        grid_spec=pltpu.PrefetchScalarGridSpec(
            num_scalar_prefetch=1, grid=(n_steps,),
            in_specs=[cast.in_spec(step_of)], out_specs=cast.out_specs(step_of)),
        out_shape=cast.out_shapes(),
        compiler_params=_params(1),
        name="cast_w_in",
    )(l, cast.w)


def _qkv_body(l_ref, x_ref, w_ref, c_ref, s1_ref, s2_ref, o_ref, *, attn_width, kv_width):
    x = x_ref[...]
    half = ROT_DIM // 2

    def rope(a):
        return (a * c_ref[...] + pltpu.roll(a, half, 1) * s1_ref[...]
                + pltpu.roll(a, LANES - half, 1) * s2_ref[...])

    q_scale = HEAD_DIM ** -0.5 * LOG2E
    width = o_ref.shape[1]
    for c in range(width // CHUNK):
        acc = _dot(x, w_ref[:, c * CHUNK:(c + 1) * CHUNK])
        for s in range(CHUNK // LANES):
            col = c * CHUNK + s * LANES
            a = acc[:, s * LANES:(s + 1) * LANES]
            if col < attn_width:
                a = rope(a) * q_scale
            elif col < attn_width + kv_width:
                a = rope(a)
            o_ref[:, col:col + LANES] = a.astype(o_ref.dtype)


def _qkv(l, x, w_qkv, rope_c, rope_s1, rope_s2, casts, *, attn_width, kv_width, bm):
    m, d = x.shape
    width = w_qkv.shape[1]
    assert width % CHUNK == 0 and attn_width % LANES == 0 and kv_width % LANES == 0
    row = lambda i, l: (i, 0)
    return _call_with_casts(
        functools.partial(_qkv_body, attn_width=attn_width, kv_width=kv_width), 5, casts, lambda i: i,
        grid=(m // bm,),
        in_specs=[pl.BlockSpec((bm, d), row), _resident((d, width), lambda i, l: (0, 0)),
                  pl.BlockSpec((bm, LANES), row), pl.BlockSpec((bm, LANES), row), pl.BlockSpec((bm, LANES), row)],
        out_spec=pl.BlockSpec((bm, width), row),
        out_shape=jax.ShapeDtypeStruct((m, width), BF16),
        name="qkv", operands=(l, x, w_qkv, rope_c, rope_s1, rope_s2))


def _conv_body(l_ref, x_ref, w_ref, cw_ref, o_ref, halo_ref, *, tiles_per_seq):
    i = pl.program_id(0)
    bm, width = o_ref.shape

    @pl.when(i % tiles_per_seq == 0)
    def _():
        halo_ref[...] = jnp.zeros_like(halo_ref)

    x = x_ref[...]
    cw = cw_ref[...]
    row = lax.broadcasted_iota(jnp.int32, (bm, CHUNK), 0)
    for c in range(width // CHUNK):
        sl = slice(c * CHUNK, (c + 1) * CHUNK)
        gate_b = _dot(x, w_ref[:, sl])
        u = _dot(x, w_ref[:, width + c * CHUNK:width + (c + 1) * CHUNK]) * \
            _dot(x, w_ref[:, 2 * width + c * CHUNK:2 * width + (c + 1) * CHUNK])
        prev = halo_ref[:, sl]
        halo_ref[:, sl] = u[bm - SUBLANES:, :]
        conv = cw[CONV_K - 1:CONV_K, sl] * u
        for back in range(1, CONV_K):
            shifted = pltpu.roll(u, back, 0)
            for t in range(back):
                src = SUBLANES - back + t
                shifted = jnp.where(row == t, prev[src:src + 1, :], shifted)
            conv = conv + cw[CONV_K - 1 - back:CONV_K - back, sl] * shifted
        o_ref[:, sl] = (gate_b * conv).astype(o_ref.dtype)


def _short_conv(l, x, w_conv, conv_w, casts, *, seq, bm):
    m, d = x.shape
    width = conv_w.shape[2]
    assert seq % bm == 0 and width % CHUNK == 0 and w_conv.shape[1] == 3 * width
    return _call_with_casts(
        functools.partial(_conv_body, tiles_per_seq=seq // bm), 3, casts, lambda i: i,
        grid=(m // bm,),
        in_specs=[pl.BlockSpec((bm, d), lambda i, l: (i, 0)), _resident((d, 3 * width), lambda i, l: (0, 0)),
                  _resident((None, CONV_K, width), lambda i, l: (l[0], 0, 0))],
        out_spec=pl.BlockSpec((bm, width), lambda i, l: (i, 0)),
        out_shape=jax.ShapeDtypeStruct((m, width), BF16),
        scratch_shapes=[pltpu.VMEM((SUBLANES, width), F32)],
        name="short_conv", operands=(l, x, w_conv, conv_w))


def _mm_body(l_ref, x_ref, w_ref, o_ref, *, epilogue):
    x = x_ref[...]
    for c in range(o_ref.shape[1] // CHUNK):
        sl = slice(c * CHUNK, (c + 1) * CHUNK)
        o_ref[:, sl] = epilogue(_dot(x, w_ref[:, sl])).astype(o_ref.dtype)


def _relu_sq(a):
    r = jnp.maximum(a, 0.0)
    return r * r


def _matmul(l, x, w, epilogue, casts, *, bm, bn, name):
    m, d = x.shape
    n = w.shape[1]
    assert bn % CHUNK == 0
    n_j = n // bn
    return _call_with_casts(
        functools.partial(_mm_body, epilogue=epilogue), 2, casts, lambda i, j: i * n_j + j,
        grid=(m // bm, n_j),
        in_specs=[pl.BlockSpec((bm, d), lambda i, j, l: (i, 0)), pl.BlockSpec((d, bn), lambda i, j, l: (0, j))],
        out_spec=pl.BlockSpec((bm, bn), lambda i, j, l: (i, j)),
        out_shape=jax.ShapeDtypeStruct((m, n), BF16),
        name=name, operands=(l, x, w))


def _attn_kernel(l_ref, q_ref, kvo_ref, kvp_ref, sink_ref, o_ref, *, nblk, n_kv_heads):
    r = pl.program_id(0)
    n = r % nblk
    layer = l_ref[0]
    kv_w = n_kv_heads * HEAD_DIM
    n_keys = 2 * BLOCK
    wide = Q_GROUP * BLOCK
    slabs_per_kv = KV_RATIO // 2
    kv = jnp.concatenate([kvp_ref[...], kvo_ref[...]], axis=0).astype(F32)

    key_row = lax.broadcasted_iota(jnp.int32, (BLOCK, wide), 0)
    query = lax.broadcasted_iota(jnp.int32, (BLOCK, wide), 1) & (BLOCK - 1)
    from_prev = key_row > query
    prev_bias = jnp.where(n > 0, 0.0, NEG_INF)
    col_slab = lax.broadcasted_iota(jnp.int32, (1, wide), 1) // BLOCK
    low_lane = lax.broadcasted_iota(jnp.int32, (n_keys, LANES), 1) < HEAD_DIM
    low_row = lax.broadcasted_iota(jnp.int32, (LANES, n_keys), 0) < HEAD_DIM

    def place(t, low, src_half, dst_half, axis):
        moved = t if src_half == dst_half else pltpu.roll(t, HEAD_DIM, axis)
        keep = jnp.logical_not(low) if dst_half else low
        return jnp.where(keep, moved, 0.0).astype(BF16)

    for kvh in range(n_kv_heads):
        src = kvh % 2
        ks = kv[:, (kvh // 2) * LANES:(kvh // 2 + 1) * LANES]
        vs_t = kv[:, kv_w + (kvh // 2) * LANES:kv_w + (kvh // 2 + 1) * LANES].T
        kz = [place(ks, low_lane, src, p, 1) for p in range(2)]
        vz_t = [place(vs_t, low_row, src, p, 0) for p in range(2)]
        for first in range(0, slabs_per_kv, Q_GROUP):
            slabs = [kvh * slabs_per_kv + first + t for t in range(Q_GROUP)]
            q_rows = jnp.concatenate([q_ref[:, s * LANES:(s + 1) * LANES] for s in slabs], axis=0)
            out_t = None
            for p in range(2):
                sink = jnp.zeros((1, wide), F32)
                for t, s in enumerate(slabs):
                    sink = jnp.where(col_slab == t, sink_ref[layer, 2 * s + p] * LOG2E, sink)
                s_t = lax.dot_general(kz[p], q_rows, (((1,), (1,)), ((), ())), preferred_element_type=F32)
                band = jnp.where(from_prev, s_t[:BLOCK] + prev_bias, s_t[BLOCK:])
                mx = jnp.maximum(jnp.max(band, axis=0, keepdims=True), sink)
                e = jnp.exp2(band - mx)
                den = jnp.sum(e, axis=0, keepdims=True) + jnp.exp2(sink - mx)
                e_keys = jnp.concatenate([jnp.where(from_prev, e, 0.0), jnp.where(from_prev, 0.0, e)], axis=0)
                pv_t = _dot(vz_t[p], e_keys.astype(BF16))
                term = pv_t * (1.0 / den)
                out_t = term if out_t is None else out_t + term
            for t, s in enumerate(slabs):
                o_ref[:, s * LANES:(s + 1) * LANES] = out_t[:, t * BLOCK:(t + 1) * BLOCK].T.astype(o_ref.dtype)


def _attention(l, qkv, sinks, *, seq, attn_width, kv_width):
    m = qkv.shape[0]
    nblk = seq // BLOCK
    n_kv_heads = kv_width // HEAD_DIM
    assert n_kv_heads % 2 == 0 and attn_width == n_kv_heads * KV_RATIO * HEAD_DIM
    assert attn_width % (2 * kv_width) == 0
    kv_col = attn_width // (2 * kv_width)
    return pl.pallas_call(
        functools.partial(_attn_kernel, nblk=nblk, n_kv_heads=n_kv_heads),
        grid_spec=pltpu.PrefetchScalarGridSpec(
            num_scalar_prefetch=1,
            grid=(m // BLOCK,),
            in_specs=[
                pl.BlockSpec((BLOCK, attn_width), lambda r, l: (r, 0)),
                pl.BlockSpec((BLOCK, 2 * kv_width), lambda r, l: (r, kv_col)),
                pl.BlockSpec((BLOCK, 2 * kv_width), lambda r, l: (jnp.maximum(r - 1, 0), kv_col)),
                pl.BlockSpec(memory_space=pltpu.SMEM),
            ],
            out_specs=pl.BlockSpec((BLOCK, attn_width), lambda r, l: (r, 0)),
        ),
        out_shape=jax.ShapeDtypeStruct((m, attn_width), BF16),
        compiler_params=_params(1),
        name="attn",
    )(l, qkv, qkv, qkv, sinks)


def _merge_kernel(attn_ref, conv_ref, wa_ref, wc_ref, ga_ref, gc_ref, o_ref):
    a_in = attn_ref[...]
    c_in = conv_ref[...]
    for c in range(o_ref.shape[1] // CHUNK):
        sl = slice(c * CHUNK, (c + 1) * CHUNK)
        a = _dot(a_in, wa_ref[:, sl])
        cv = _dot(c_in, wc_ref[:, sl])
        o_ref[:, sl] = (ga_ref[:, sl].astype(F32) * a + gc_ref[:, sl].astype(F32) * cv).astype(o_ref.dtype)


def _merge(attn, conv, gates, w_br_attn, w_br_conv, *, bm):
    m, attn_width = attn.shape
    conv_width = conv.shape[1]
    d = w_br_attn.shape[1]
    assert gates.shape[1] == 2 * d and d % CHUNK == 0
    return pl.pallas_call(
        _merge_kernel,
        grid=(m // bm,),
        in_specs=[
            pl.BlockSpec((bm, attn_width), lambda i: (i, 0)),
            pl.BlockSpec((bm, conv_width), lambda i: (i, 0)),
            _resident((attn_width, d), lambda i: (0, 0)),
            _resident((conv_width, d), lambda i: (0, 0)),
            pl.BlockSpec((bm, d), lambda i: (i, 0)),
            pl.BlockSpec((bm, d), lambda i: (i, 1)),
        ],
        out_specs=pl.BlockSpec((bm, d), lambda i: (i, 0)),
        out_shape=jax.ShapeDtypeStruct((m, d), BF16),
        compiler_params=_params(1),
        name="merge",
    )(attn, conv, w_br_attn, w_br_conv, gates, gates)


def _layer_norm_store(load_chunk, n_chunks, g_ref, b_ref, o_ref):
    d = o_ref.shape[1]
    w = d // n_chunks
    means, m2s = [], []
    for c in range(n_chunks):
        y = load_chunk(c)
        mean_c = y.sum(axis=1, keepdims=True) * (1.0 / w)
        means.append(mean_c)
        m2s.append(jnp.square(y - mean_c).sum(axis=1, keepdims=True))
    mu = sum(means[1:], means[0]) * (1.0 / n_chunks)
    m2 = sum(m2s[1:], m2s[0]) + w * sum((jnp.square(mc - mu) for mc in means[1:]), jnp.square(means[0] - mu))
    rstd = lax.rsqrt(m2 * (1.0 / d) + LN_EPS)
    for c in range(n_chunks):
        sl = slice(c * w, (c + 1) * w)
        o_ref[:, sl] = ((load_chunk(c) - mu) * rstd * g_ref[:, sl] + b_ref[:, sl]).astype(o_ref.dtype)


def _oproj_kernel(l_ref, a_ref, w_ref, xres_ref, g_ref, b_ref, o_ref, y0_ref, y1_ref, *, alpha, n_tiles):
    i = pl.program_id(0)
    y_refs = (y0_ref, y1_ref)
    n_chunks = y0_ref.shape[0]

    def project(y_ref):
        a_in = a_ref[...]
        for c in range(n_chunks):
            sl = slice(c * CHUNK, (c + 1) * CHUNK)
            y_ref[c] = alpha * xres_ref[:, sl].astype(F32) + _dot(a_in, w_ref[:, sl])

    def normalize(y_ref):
        _layer_norm_store(lambda c: y_ref[c], n_chunks, g_ref, b_ref, o_ref)

    @pl.when(i == 0)
    def _():
        project(y_refs[0])

    for parity in range(2):
        @pl.when((i > 0) & (i < n_tiles) & (i % 2 == parity))
        def _():
            normalize(y_refs[1 - parity])
            project(y_refs[parity])

    @pl.when(i == n_tiles)
    def _():
        normalize(y_refs[(n_tiles - 1) % 2])


def _oproj_ln(l, merged, w_o, xres, g, b, *, alpha, bm):
    m, d = merged.shape
    n_tiles = m // bm
    assert d % CHUNK == 0
    cur = lambda i, l: (jnp.minimum(i, n_tiles - 1), 0)
    return pl.pallas_call(
        functools.partial(_oproj_kernel, alpha=alpha, n_tiles=n_tiles),
        grid_spec=pltpu.PrefetchScalarGridSpec(
            num_scalar_prefetch=1,
            grid=(n_tiles + 1,),
            in_specs=[
                pl.BlockSpec((bm, d), cur),
                _resident((d, d), lambda i, l: (0, 0)),
                pl.BlockSpec((bm, d), cur),
                pl.BlockSpec((None, 1, d), lambda i, l: (l[0], 0, 0)),
                pl.BlockSpec((None, 1, d), lambda i, l: (l[0], 0, 0)),
            ],
            out_specs=pl.BlockSpec((bm, d), lambda i, l: (jnp.maximum(i - 1, 0), 0)),
            scratch_shapes=[pltpu.VMEM((d // CHUNK, bm, CHUNK), F32)] * 2,
        ),
        out_shape=jax.ShapeDtypeStruct((m, d), BF16),
        compiler_params=_params(1),
        name="oproj_ln",
    )(l, merged, w_o, xres, g, b)


def _down_kernel(l_ref, h_ref, w_ref, xres_ref, g_ref, b_ref, o_ref, acc_ref, *, alpha, n_k):
    k = pl.program_id(1)
    n_chunks = acc_ref.shape[0]

    def accumulate(first):
        h = h_ref[...]
        for c in range(n_chunks):
            sl = slice(c * CHUNK, (c + 1) * CHUNK)
            base = alpha * xres_ref[:, sl].astype(F32) if first else acc_ref[c]
            acc_ref[c] = base + _dot(h, w_ref[:, sl])

    @pl.when(k == 0)
    def _():
        accumulate(True)

    @pl.when((k > 0) & (k < n_k - 1))
    def _():
        accumulate(False)

    @pl.when(k == n_k - 1)
    def _():
        accumulate(False)
        _layer_norm_store(lambda c: acc_ref[c], n_chunks, g_ref, b_ref, o_ref)


def _down_ln(l, hidden, w_down, xres, g, b, *, alpha, bm, bk):
    m, d_ff = hidden.shape
    d = xres.shape[1]
    n_k = d_ff // bk
    assert d % CHUNK == 0 and n_k >= 2
    return pl.pallas_call(
        functools.partial(_down_kernel, alpha=alpha, n_k=n_k),
        grid_spec=pltpu.PrefetchScalarGridSpec(
            num_scalar_prefetch=1,
            grid=(m // bm, n_k),
            in_specs=[
                pl.BlockSpec((bm, bk), lambda i, k, l: (i, k)),
                pl.BlockSpec((bk, d), lambda i, k, l: (k, 0)),
                pl.BlockSpec((bm, d), lambda i, k, l: (i, 0)),
                pl.BlockSpec((None, 1, d), lambda i, k, l: (l[0], 0, 0)),
                pl.BlockSpec((None, 1, d), lambda i, k, l: (l[0], 0, 0)),
            ],
            out_specs=pl.BlockSpec((bm, d), lambda i, k, l: (i, 0)),
            scratch_shapes=[pltpu.VMEM((d // CHUNK, bm, CHUNK), F32)],
        ),
        out_shape=jax.ShapeDtypeStruct((m, d), BF16),
        compiler_params=_params(2),
        name="down_ln",
    )(l, hidden, w_down, xres, g, b)


def _rope_lane_tables(positions):
    half = ROT_DIM // 2
    inv_freq = ROPE_THETA ** (-jnp.arange(0, ROT_DIM, 2, dtype=F32) / ROT_DIM)
    ang = positions.astype(F32).reshape(-1, 1) * inv_freq
    cos, sin = jnp.cos(ang), jnp.sin(ang)
    m = ang.shape[0]
    zeros = lambda w: jnp.zeros((m, w), F32)
    c64 = jnp.concatenate([cos, cos, jnp.ones((m, HEAD_DIM - ROT_DIM), F32)], axis=1)
    s1_64 = jnp.concatenate([zeros(half), sin, zeros(HEAD_DIM - ROT_DIM)], axis=1)
    s2_64 = jnp.concatenate([-sin, zeros(HEAD_DIM - half)], axis=1)
    rep = LANES // HEAD_DIM
    return jnp.tile(c64, (1, rep)), jnp.tile(s1_64, (1, rep)), jnp.tile(s2_64, (1, rep))


def kernel(x, positions, w_in, conv_w, attn_sinks, w_br_attn, w_br_conv, w_o, ln1_g, ln1_b, w_up, w_down, ln2_g, ln2_b):
    batch, seq, d_model = x.shape
    depth = w_in.shape[0]
    m = batch * seq
    attn_width = w_br_attn.shape[1]
    conv_width = w_br_conv.shape[1]
    d_ff = w_up.shape[2]
    kv_width = attn_width // KV_RATIO
    qkv_width = attn_width + 2 * kv_width
    in_splits = (qkv_width, 3 * conv_width, 2 * d_model)
    assert w_in.shape[2] == sum(in_splits)
    alpha = (2 * depth) ** 0.25
    bm_big = min(1024, seq)
    bm_mid = min(512, seq)
    bm_small = min(256, seq)
    bn = 1024
    steps_mid = m // bm_mid
    steps_gates = (m // bm_big) * (2 * d_model // bn)
    steps_up = (m // bm_big) * (d_ff // bn)

    rope_c, rope_s1, rope_s2 = lax.optimization_barrier(_rope_lane_tables(positions))
    g1, b1, g2, b2 = (t.reshape(depth, 1, d_model) for t in (ln1_g, ln1_b, ln2_g, ln2_b))
    steps_w0 = d_model // 128
    w_in0 = tuple(_cast_only(jnp.zeros((1,), jnp.int32), _Cast(w_in, steps_w0, in_splits), steps_w0))
    next_layer = lambda l: jnp.minimum(l[0] + 1, depth - 1)

    def layer(li, carry):
        xs, w_qkv, w_conv, w_gate = carry
        l = jnp.reshape(li, (1,)).astype(jnp.int32)
        qkv, w_ba = _qkv(l, xs, w_qkv, rope_c, rope_s1, rope_s2, [_Cast(w_br_attn, steps_mid)],
                         attn_width=attn_width, kv_width=kv_width, bm=bm_mid)
        conv, w_bc, w_op = _short_conv(l, xs, w_conv, conv_w, [_Cast(w_br_conv, steps_mid), _Cast(w_o, steps_mid)],
                                       seq=seq, bm=bm_mid)
        gates, w_u = _matmul(l, xs, w_gate, jax.nn.sigmoid, [_Cast(w_up, steps_gates)], bm=bm_big, bn=bn,
                             name="gates")
        attn = _attention(l, qkv, attn_sinks, seq=seq, attn_width=attn_width, kv_width=kv_width)
        merged = _merge(attn, conv, gates, w_ba, w_bc, bm=bm_small)
        x1 = _oproj_ln(l, merged, w_op, xs, g1, b1, alpha=alpha, bm=bm_small)
        hidden, w_d, w_qkv_n, w_conv_n, w_gate_n = _matmul(
            l, x1, w_u, _relu_sq, [_Cast(w_down, steps_up), _Cast(w_in, steps_up, in_splits, next_layer)],
            bm=bm_big, bn=bn, name="up")
        x2 = _down_ln(l, hidden, w_d, x1, g2, b2, alpha=alpha, bm=bm_mid, bk=bn)
        return x2, w_qkv_n, w_conv_n, w_gate_n

    xs, *_ = lax.fori_loop(0, depth, layer, (x.reshape(m, d_model).astype(BF16),) + w_in0)
    return xs.astype(F32).reshape(batch, seq, d_model)
```
